```python
import math
import jax, jax.numpy as jnp
from jax import lax
import numpy as np


D_MODEL = 1024
BATCH = 8
SEQ = 2048
DEPTH = 1
DEC_BATCH = 128
DEC_SEQ = 4
PAST_LEN = 16384
PAGE_SIZE = 128

N_HEADS_A = 4
D_A = D_MODEL
HEAD_DIM = D_A // N_HEADS_A
CHUNK = 64
D_B = D_MODEL
CONV_WIDTH = 3
N_KEYS = 128
N_EXPERTS = N_KEYS * N_KEYS
N_RHEADS = 8
D_KEY = 256
D_HALF = D_KEY // 2
TOPK = 16
TOKEN_BLOCK = 128
EPS = 1e-6
IN_COLS = 4 * D_A + 2 * N_HEADS_A + 3 * D_B + 2 * D_MODEL

kernel_name = "hybrid_mlstm_shortconv_peer_step"


def _split_points():
    sizes = (D_A, D_A, D_A, D_A, N_HEADS_A, N_HEADS_A, D_B, D_B, D_B, D_MODEL, D_MODEL)
    return tuple(np.cumsum(sizes)[:-1].tolist())


def rmsnorm(x, g):
    x32 = x.astype(jnp.float32)
    y = x32 * lax.rsqrt(jnp.mean(x32 * x32, axis=-1, keepdims=True) + EPS)
    return (y * g.astype(jnp.float32)).astype(x.dtype)


def mlstm_chunkwise(q, k, v, i_pre, logf, c0, n0, m0):
    b, s, h, d = q.shape
    L = math.gcd(s, CHUNK)
    nc = s // L

    def chunks_vec(t):
        return t.astype(jnp.float32).reshape(b, nc, L, h, d).transpose(1, 0, 3, 2, 4)

    def chunks_gate(t):
        return t.reshape(b, nc, L, h).transpose(1, 0, 3, 2)

    causal = jnp.tril(jnp.ones((L, L), dtype=bool))

    def step(carry, xs):
        c, n, m = carry
        qc, kc, vc, ic, fc = xs
        bcum = jnp.cumsum(fc, axis=-1)
        dmat = bcum[..., :, None] - bcum[..., None, :] + ic[..., None, :]
        dmat = jnp.where(causal, dmat, -jnp.inf)
        inter = bcum + m[..., None]
        m_row = jnp.maximum(inter, jnp.max(dmat, axis=-1))
        w = jnp.exp(dmat - m_row[..., None])
        a = jnp.exp(inter - m_row)
        sw = jnp.einsum('bhtd,bhsd->bhts', qc, kc) * w
        num = a[..., None] * jnp.einsum('bhvk,bhtk->bhtv', c, qc) + jnp.einsum('bhts,bhsv->bhtv', sw, vc)
        den = a * jnp.einsum('bhk,bhtk->bht', n, qc) + jnp.sum(sw, axis=-1)
        hout = num / jnp.maximum(jnp.abs(den), jnp.exp(-m_row))[..., None]
        m_new = m_row[..., -1]
        decay = jnp.exp(bcum[..., -1:] - bcum + ic - m_new[..., None])
        a_end = jnp.exp(bcum[..., -1] + m - m_new)
        c_new = a_end[..., None, None] * c + jnp.einsum('bhs,bhsv,bhsk->bhvk', decay, vc, kc)
        n_new = a_end[..., None] * n + jnp.einsum('bhs,bhsk->bhk', decay, kc)
        return (c_new, n_new, m_new), hout

    carry0 = (c0.astype(jnp.float32), n0.astype(jnp.float32), m0.astype(jnp.float32))
    xs = (chunks_vec(q), chunks_vec(k), chunks_vec(v), chunks_gate(i_pre), chunks_gate(logf))
    (c1, n1, m1), hs = lax.scan(step, carry0, xs)
    hs = hs.transpose(1, 0, 3, 2, 4).reshape(b, s, h, d)
    return hs, c1, n1, m1


def short_conv(u, buf, w):
    s = u.shape[1]
    ext = jnp.concatenate([buf.astype(u.dtype), u], axis=1)
    y = ext[:, 0:s] * w[0]
    for j in range(1, CONV_WIDTH):
        y = y + ext[:, j:j + s] * w[j]
    return y, ext[:, s:]


def hybrid_mixer(h, c0, n0, m0, buf0, w_in, b_if, head_norm, w_out_a, w_out_b, conv_w, w_o):
    bsz, s, _ = h.shape
    proj = jnp.einsum('bsd,dc->bsc', h, w_in)
    q, k, v, o, ig, fg, cb, cc, cx, ga, gb = jnp.split(proj, _split_points(), axis=-1)

    def heads(t):
        return t.reshape(bsz, s, N_HEADS_A, HEAD_DIM)

    i_pre = (ig + b_if[:N_HEADS_A]).astype(jnp.float32)
    logf = jax.nn.log_sigmoid((fg + b_if[N_HEADS_A:]).astype(jnp.float32))
    hcell, c1, n1, m1 = mlstm_chunkwise(heads(q), heads(k) * (HEAD_DIM ** -0.5), heads(v), i_pre, logf, c0, n0, m0)
    hn = hcell * lax.rsqrt(jnp.mean(hcell * hcell, axis=-1, keepdims=True) + EPS) * head_norm.astype(jnp.float32)
    ha = (jax.nn.sigmoid(heads(o).astype(jnp.float32)) * hn).astype(h.dtype).reshape(bsz, s, D_A)
    ya = jnp.einsum('bsc,cd->bsd', ha, w_out_a)

    conv_y, buf1 = short_conv(cc * cx, buf0, conv_w)
    yb = jnp.einsum('bsc,cd->bsd', cb * conv_y, w_out_b)

    merged = jax.nn.sigmoid(ga) * ya + jax.nn.sigmoid(gb) * yb
    out = jnp.einsum('bsd,de->bse', merged, w_o)
    return out, c1, n1, m1, buf1


def peer(h, w_query, keys1, keys2, expert_u, expert_v):
    shp = h.shape
    t = h.reshape(-1, D_MODEL)
    n_tok = t.shape[0]
    pad = (-n_tok) % TOKEN_BLOCK
    t = jnp.pad(t, ((0, pad), (0, 0)))
    blocks = t.reshape(-1, TOKEN_BLOCK, D_MODEL)

    def block_fn(xb):
        qb = jnp.einsum('td,dc->tc', xb, w_query).reshape(TOKEN_BLOCK, N_RHEADS, 2, D_HALF)
        s1 = jnp.einsum('thd,hkd->thk', qb[:, :, 0], keys1).astype(jnp.float32)
        s2 = jnp.einsum('thd,hkd->thk', qb[:, :, 1], keys2).astype(jnp.float32)
        v1, i1 = lax.top_k(s1, TOPK)
        v2, i2 = lax.top_k(s2, TOPK)
        cand = (v1[..., :, None] + v2[..., None, :]).reshape(TOKEN_BLOCK, N_RHEADS, TOPK * TOPK)
        cand_idx = (i1[..., :, None] * N_KEYS + i2[..., None, :]).reshape(TOKEN_BLOCK, N_RHEADS, TOPK * TOPK)
        best, pos = lax.top_k(cand, TOPK)
        idx = jnp.take_along_axis(cand_idx, pos, axis=-1)
        gate = jax.nn.softmax(best, axis=-1)
        u = expert_u[idx]
        act = jax.nn.gelu(jnp.einsum('thkd,td->thk', u, xb).astype(jnp.float32), approximate=False)
        vv = expert_v[idx]
        return jnp.einsum('thk,thkd->td', (gate * act).astype(xb.dtype), vv)

    out = lax.map(block_fn, blocks).reshape(-1, D_MODEL)[:n_tok]
    return out.reshape(shp)


def setup_inputs(seed: int = 0) -> dict:
    key = jax.random.key(seed)
    ks = jax.random.split(key, 24)
    f32 = jnp.float32
    nrm = lambda k, shp, sc: jax.random.normal(k, shp, f32) * sc
    return {
        "x_prompt": nrm(ks[0], (BATCH, SEQ, D_MODEL), 1.0),
        "x_sample": nrm(ks[1], (DEC_BATCH, DEC_SEQ, D_MODEL), 1.0),
        "state_mlstm_C": nrm(ks[2], (DEPTH, DEC_BATCH, N_HEADS_A, HEAD_DIM, HEAD_DIM), HEAD_DIM ** -0.5),
        "state_mlstm_n": nrm(ks[3], (DEPTH, DEC_BATCH, N_HEADS_A, HEAD_DIM), 0.1),
        "state_mlstm_m": nrm(ks[4], (DEPTH, DEC_BATCH, N_HEADS_A), 1.0),
        "state_conv": nrm(ks[5], (DEPTH, DEC_BATCH, CONV_WIDTH - 1, D_B), 1.0),
        "norm_mix": 1.0 + nrm(ks[6], (DEPTH, D_MODEL), 0.05),
        "w_in": nrm(ks[7], (DEPTH, D_MODEL, IN_COLS), D_MODEL ** -0.5),
        "b_if": jnp.concatenate([nrm(ks[8], (DEPTH, N_HEADS_A), 0.1),
                                 3.0 + nrm(ks[9], (DEPTH, N_HEADS_A), 0.5)], axis=-1),
        "head_norm": 1.0 + nrm(ks[10], (DEPTH, N_HEADS_A, HEAD_DIM), 0.05),
        "w_out_a": nrm(ks[11], (DEPTH, D_A, D_MODEL), D_A ** -0.5),
        "w_out_b": nrm(ks[12], (DEPTH, D_B, D_MODEL), D_B ** -0.5),
        "conv_w": nrm(ks[13], (DEPTH, CONV_WIDTH, D_B), CONV_WIDTH ** -0.5),
        "w_o": nrm(ks[14], (DEPTH, D_MODEL, D_MODEL), D_MODEL ** -0.5),
        "norm_ffn": 1.0 + nrm(ks[15], (DEPTH, D_MODEL), 0.05),
        "w_query": nrm(ks[16], (DEPTH, D_MODEL, N_RHEADS * D_KEY), D_MODEL ** -0.5),
        "sub_keys1": nrm(ks[17], (DEPTH, N_RHEADS, N_KEYS, D_HALF), D_HALF ** -0.5),
        "sub_keys2": nrm(ks[18], (DEPTH, N_RHEADS, N_KEYS, D_HALF), D_HALF ** -0.5),
        "expert_u": nrm(ks[19], (DEPTH, N_EXPERTS, D_MODEL), D_MODEL ** -0.5),
        "expert_v": nrm(ks[20], (DEPTH, N_EXPERTS, D_MODEL), N_RHEADS ** -0.5),
        "norm_final": 1.0 + nrm(ks[21], (D_MODEL,), 0.05),
    }


def reference(x_prompt, x_sample, state_mlstm_C, state_mlstm_n, state_mlstm_m, state_conv,
              norm_mix, w_in, b_if, head_norm, w_out_a, w_out_b, conv_w, w_o,
              norm_ffn, w_query, sub_keys1, sub_keys2, expert_u, expert_v, norm_final):
    xp, xs = x_prompt, x_sample
    cp_l, np_l, mp_l, bp_l = [], [], [], []
    cs_l, ns_l, ms_l, bs_l = [], [], [], []
    for l in range(DEPTH):
        mix_w = (w_in[l], b_if[l], head_norm[l], w_out_a[l], w_out_b[l], conv_w[l], w_o[l])
        hp = rmsnorm(xp, norm_mix[l])
        yp, cp, np_, mp, bp = hybrid_mixer(
            hp,
            jnp.zeros((BATCH, N_HEADS_A, HEAD_DIM, HEAD_DIM), jnp.float32),
            jnp.zeros((BATCH, N_HEADS_A, HEAD_DIM), jnp.float32),
            jnp.zeros((BATCH, N_HEADS_A), jnp.float32),
            jnp.zeros((BATCH, CONV_WIDTH - 1, D_B), hp.dtype),
            *mix_w)
        xp = xp + yp
        hs = rmsnorm(xs, norm_mix[l])
        ys, cs, ns, ms, bs = hybrid_mixer(hs, state_mlstm_C[l], state_mlstm_n[l], state_mlstm_m[l],
                                          state_conv[l], *mix_w)
        xs = xs + ys
        peer_w = (w_query[l], sub_keys1[l], sub_keys2[l], expert_u[l], expert_v[l])
        xp = xp + peer(rmsnorm(xp, norm_ffn[l]), *peer_w)
        xs = xs + peer(rmsnorm(xs, norm_ffn[l]), *peer_w)
        cp_l.append(cp.astype(state_mlstm_C.dtype)); np_l.append(np_.astype(state_mlstm_n.dtype))
        mp_l.append(mp.astype(state_mlstm_m.dtype)); bp_l.append(bp.astype(state_conv.dtype))
        cs_l.append(cs.astype(state_mlstm_C.dtype)); ns_l.append(ns.astype(state_mlstm_n.dtype))
        ms_l.append(ms.astype(state_mlstm_m.dtype)); bs_l.append(bs.astype(state_conv.dtype))
    y_prompt = rmsnorm(xp, norm_final)
    y_sample = rmsnorm(xs, norm_final)
    c_p = jnp.stack(cp_l, 0)
    n_p = jnp.stack(np_l, 0)
    m_p = jnp.stack(mp_l, 0)
    conv_p = jnp.stack(bp_l, 0)
    c_s = jnp.stack(cs_l, 0)
    n_s = jnp.stack(ns_l, 0)
    m_s = jnp.stack(ms_l, 0)
    conv_s = jnp.stack(bs_l, 0)
    return (y_prompt, y_sample, c_p, n_p, m_p, conv_p, c_s, n_s, m_s, conv_s)
```

```python
import functools
import math

import jax
import jax.numpy as jnp
from jax import lax
from jax.experimental import pallas as pl
from jax.experimental.pallas import tpu as pltpu

F32 = jnp.float32
BF16 = jnp.bfloat16
EPS = 1e-6
TOPK = 16
LANES = 128
SUBLANES = 8
V7X_VMEM_BYTES = 64 * 1024 * 1024
VMEM_LIMIT_BYTES = V7X_VMEM_BYTES * 7 // 8
MLSTM_CHUNK = 256
NEG_INF = float("-inf")


def _cparams(*semantics):
    return pltpu.CompilerParams(dimension_semantics=semantics, vmem_limit_bytes=VMEM_LIMIT_BYTES)


def _dot(a, b):
    return jnp.dot(a, b, preferred_element_type=F32)


def _dot_nt(a, b):
    return lax.dot_general(a, b, (((1,), (1,)), ((), ())), preferred_element_type=F32)


def _rmsnorm(x, g):
    return x * lax.rsqrt(jnp.mean(x * x, axis=-1, keepdims=True) + EPS) * g


def _sigmoid(x):
    return 1.0 / (1.0 + jnp.exp(-x))


def _log_sigmoid(x):
    return jnp.minimum(x, 0.0) - jnp.log1p(jnp.exp(-jnp.abs(x)))


def _split_bf16(x, parts):
    out = []
    for _ in range(parts - 1):
        p = x.astype(BF16)
        out.append(p)
        x = x - p.astype(F32)
    out.append(x.astype(BF16))
    return out


def _tile(n, pref):
    t = math.gcd(n, pref)
    assert t % SUBLANES == 0, (n, pref)
    return t


def _in_proj_body(x_ref, g_ref, w_ref, wif_ref, wift_ref, bcol_ref, brow_ref, proj_ref, gcol_ref, grow_ref, hn_ref, *, nh):
    @pl.when(pl.program_id(1) == 0)
    def _():
        hn = _rmsnorm(x_ref[...], g_ref[...])
        hn_ref[...] = hn.astype(BF16)
        h_hi, h_lo = _split_bf16(hn, 2)
        w_hi, w_lo = _split_bf16(wif_ref[...], 2)
        col = _dot(h_hi, w_hi) + _dot(h_hi, w_lo) + _dot(h_lo, w_hi) + bcol_ref[...]
        lane = lax.broadcasted_iota(jnp.int32, col.shape, 1)
        gcol_ref[...] = jnp.where(lane < nh, col, _log_sigmoid(col))
        t_hi, t_lo = _split_bf16(wift_ref[...], 2)
        row = _dot_nt(t_hi, h_hi) + _dot_nt(t_lo, h_hi) + _dot_nt(t_hi, h_lo) + brow_ref[...]
        sub = lax.broadcasted_iota(jnp.int32, row.shape, 0)
        grow_ref[...] = jnp.where(sub < nh, row, _log_sigmoid(row))

    proj_ref[...] = _dot(hn_ref[...], w_ref[...]).astype(proj_ref.dtype)


def _in_proj(x, g, w_all, wif, wift, bcol, brow, *, nh, out_dtype):
    t, d = x.shape
    ncol = w_all.shape[1]
    tm = _tile(t, 1024)
    tn = _tile(ncol, 1024)
    return pl.pallas_call(
        functools.partial(_in_proj_body, nh=nh),
        grid=(t // tm, ncol // tn),
        in_specs=[
            pl.BlockSpec((tm, d), lambda i, j: (i, 0)),
            pl.BlockSpec((1, d), lambda i, j: (0, 0)),
            pl.BlockSpec((d, tn), lambda i, j: (0, j)),
            pl.BlockSpec((d, LANES), lambda i, j: (0, 0)),
            pl.BlockSpec((SUBLANES, d), lambda i, j: (0, 0)),
            pl.BlockSpec((1, LANES), lambda i, j: (0, 0)),
            pl.BlockSpec((SUBLANES, 1), lambda i, j: (0, 0)),
        ],
        out_specs=[
            pl.BlockSpec((tm, tn), lambda i, j: (i, j)),
            pl.BlockSpec((tm, LANES), lambda i, j: (i, 0)),
            pl.BlockSpec((SUBLANES, tm), lambda i, j: (0, i)),
        ],
        out_shape=[
            jax.ShapeDtypeStruct((t, ncol), out_dtype),
            jax.ShapeDtypeStruct((t, LANES), F32),
            jax.ShapeDtypeStruct((SUBLANES, t), F32),
        ],
        scratch_shapes=[pltpu.VMEM((tm, d), BF16)],
        compiler_params=_cparams("parallel", "arbitrary"),
        name="in_proj",
    )(x, g, w_all, wif, wift, bcol, brow)


def _head_out(num, den, mrow, o, hnorm):
    hout = num / jnp.maximum(jnp.abs(den), jnp.exp(-mrow))
    hn = hout * lax.rsqrt(jnp.mean(hout * hout, axis=-1, keepdims=True) + EPS) * hnorm
    return _sigmoid(o) * hn


def _mlstm_prompt_body(q_ref, k_ref, v_ref, o_ref, gcol_ref, grow_ref, hnorm_ref,
                       ha_ref, c_ref, n_ref, m_ref, ct_sc, n_sc, m_sc, *, chunk, nchunks, nh, scale):
    h = pl.program_id(1)
    ct_sc[...] = jnp.zeros_like(ct_sc)
    n_sc[...] = jnp.zeros_like(n_sc)
    m_sc[...] = jnp.zeros_like(m_sc)
    row_i = lax.broadcasted_iota(jnp.int32, (chunk, chunk), 0)
    col_i = lax.broadcasted_iota(jnp.int32, (chunk, chunk), 1)
    causal = col_i <= row_i
    tril = jnp.where(causal, 1.0, 0.0).astype(BF16)
    triu = jnp.where(row_i <= col_i, 1.0, 0.0).astype(BF16)
    lane = lax.broadcasted_iota(jnp.int32, (chunk, LANES), 1)
    sub = lax.broadcasted_iota(jnp.int32, (SUBLANES, chunk), 0)
    hnorm = hnorm_ref[0]

    def step(c, carry):
        r0 = pl.multiple_of(c * chunk, chunk)
        q = q_ref[0, pl.ds(r0, chunk), :]
        ks = k_ref[0, pl.ds(r0, chunk), :].astype(F32) * scale
        v = v_ref[0, pl.ds(r0, chunk), :]
        o = o_ref[0, pl.ds(r0, chunk), :].astype(F32)
        gc = gcol_ref[0, pl.ds(r0, chunk), :]
        gr = grow_ref[:, pl.ds(r0, chunk)]
        bc = sum(_dot(tril, p) for p in _split_bf16(gc, 3))
        br = sum(_dot(p, triu) for p in _split_bf16(gr, 3))
        bcol = jnp.sum(jnp.where(lane == nh + h, bc, 0.0), axis=1, keepdims=True)
        icol = jnp.sum(jnp.where(lane == h, gc, 0.0), axis=1, keepdims=True)
        brow = jnp.sum(jnp.where(sub == nh + h, br, 0.0), axis=0, keepdims=True)
        irow = jnp.sum(jnp.where(sub == h, gr, 0.0), axis=0, keepdims=True)
        m_prev = m_sc[...]
        dmat = jnp.where(causal, bcol - brow + irow, NEG_INF)
        inter = bcol + m_prev
        mrow = jnp.maximum(inter, jnp.max(dmat, axis=1, keepdims=True))
        w = jnp.exp(dmat - mrow)
        a = jnp.exp(inter - mrow)
        kt = ks.T.astype(BF16)
        sw = _dot(q, kt) * w
        ct = ct_sc[...]
        n_prev = n_sc[...]
        num = a * _dot(q, ct.astype(BF16)) + _dot(sw.astype(BF16), v)
        den = a * jnp.sum(q.astype(F32) * n_prev, axis=1, keepdims=True) + jnp.sum(sw, axis=1, keepdims=True)
        ha_ref[0, pl.ds(r0, chunk), :] = _head_out(num, den, mrow, o, hnorm).astype(ha_ref.dtype)
        m_new = mrow[chunk - 1:chunk, :]
        b_last = bcol[chunk - 1:chunk, :]
        decay = jnp.exp(b_last - bcol + icol - m_new)
        a_end = jnp.exp(b_last + m_prev - m_new)
        ct_sc[...] = a_end * ct + _dot(kt, (decay * v.astype(F32)).astype(BF16))
        n_sc[...] = a_end * n_prev + jnp.sum(decay * ks, axis=0, keepdims=True)
        m_sc[...] = m_new
        return carry

    lax.fori_loop(0, nchunks, step, 0)
    c_ref[0, 0] = ct_sc[...].T
    n_ref[0, 0] = n_sc[...]
    m_ref[0, 0] = jnp.broadcast_to(m_sc[...], (1, LANES))


def _mlstm_prompt(proj, gcol, grow, hnorm, *, b, s, nh, dh):
    chunk = _tile(s, MLSTM_CHUNK)
    proj3 = proj.reshape(b, s, proj.shape[-1])
    gcol3 = gcol.reshape(b, s, LANES)
    blk = lambda off: pl.BlockSpec((1, s, dh), lambda i, h: (i, 0, off * nh + h))
    return pl.pallas_call(
        functools.partial(_mlstm_prompt_body, chunk=chunk, nchunks=s // chunk, nh=nh, scale=dh ** -0.5),
        grid=(b, nh),
        in_specs=[
            blk(0), blk(1), blk(2), blk(3),
            pl.BlockSpec((1, s, LANES), lambda i, h: (i, 0, 0)),
            pl.BlockSpec((SUBLANES, s), lambda i, h: (0, i)),
            pl.BlockSpec((1, 1, dh), lambda i, h: (h, 0, 0)),
        ],
        out_specs=[
            pl.BlockSpec((1, s, dh), lambda i, h: (i, 0, h)),
            pl.BlockSpec((1, 1, dh, dh), lambda i, h: (i, h, 0, 0)),
            pl.BlockSpec((1, 1, 1, dh), lambda i, h: (i, h, 0, 0)),
            pl.BlockSpec((1, 1, 1, LANES), lambda i, h: (i, h, 0, 0)),
        ],
        out_shape=[
            jax.ShapeDtypeStruct((b, s, nh * dh), BF16),
            jax.ShapeDtypeStruct((b, nh, dh, dh), F32),
            jax.ShapeDtypeStruct((b, nh, 1, dh), F32),
            jax.ShapeDtypeStruct((b, nh, 1, LANES), F32),
        ],
        scratch_shapes=[pltpu.VMEM((dh, dh), F32), pltpu.VMEM((1, dh), F32), pltpu.VMEM((1, 1), F32)],
        compiler_params=_cparams("parallel", "parallel"),
        name="mlstm_prompt",
    )(proj3, proj3, proj3, proj3, gcol3, grow, hnorm)


def _mlstm_sample_body(q_ref, k_ref, v_ref, o_ref, gcol_ref, grow_ref, m0_ref, c0_ref, n0_ref, hnorm_ref,
                       ha_ref, c_ref, n_ref, m_ref, vt_sc, ks_sc, *, s, nh, dh, scale, steps_per_block):
    nseq = SUBLANES // s
    shift = s.bit_length() - 1
    jj = pl.program_id(0) % steps_per_block

    @pl.when(jj == 0)
    def _():
        vt_sc[...] = v_ref[...].astype(F32).T.astype(BF16)
        ks_sc[...] = (k_ref[...].astype(F32) * scale).astype(BF16)

    r8 = pl.multiple_of(jj * SUBLANES, SUBLANES)
    lane = lax.broadcasted_iota(jnp.int32, (SUBLANES, LANES), 1) - r8
    sub = lax.broadcasted_iota(jnp.int32, (SUBLANES, LANES), 0)
    inblk = (lane >= 0) & (lane < SUBLANES)
    same = inblk & ((lane >> shift) == (sub >> shift))
    causal = same & (lane <= sub)
    eye = lane == sub
    lane_row = lane[0:1, :]
    inblk_row = inblk[0:1, :]
    sub_col = lax.broadcasted_iota(jnp.int32, (SUBLANES, 1), 0)
    seq_of_row = lax.broadcasted_iota(jnp.int32, (SUBLANES, dh), 0) >> shift

    def to_row(col):
        return jnp.sum(jnp.where(eye, col, 0.0), axis=0, keepdims=True)

    def per_seq(col_fn):
        out = col_fn(nseq - 1)
        for bi in range(nseq - 2, -1, -1):
            out = jnp.where(sub_col >> shift == bi, col_fn(bi), out)
        return out

    gc = gcol_ref[pl.ds(r8, SUBLANES), :]
    m0 = m0_ref[pl.ds(r8, SUBLANES), :]
    for h in range(nh):
        hs = slice(h * dh, (h + 1) * dh)
        icol = gc[:, h:h + 1]
        irow = grow_ref[h:h + 1, :]
        frow = grow_ref[nh + h:nh + h + 1, :]
        m_prev = m0[:, h:h + 1]
        bcol = jnp.sum(jnp.where(causal, frow, 0.0), axis=1, keepdims=True)
        tot = jnp.sum(jnp.where(same, frow, 0.0), axis=1, keepdims=True)
        brow = to_row(bcol)
        dmat = jnp.where(causal, bcol - brow + irow, NEG_INF)
        inter = bcol + m_prev
        mrow = jnp.maximum(inter, jnp.max(dmat, axis=1, keepdims=True))
        w = jnp.exp(dmat - mrow)
        a = jnp.exp(inter - mrow)
        q = q_ref[pl.ds(r8, SUBLANES), hs].astype(BF16)
        o = o_ref[pl.ds(r8, SUBLANES), hs].astype(F32)
        ks_h = ks_sc[:, hs]
        sw = _dot_nt(q, ks_h) * w
        intra = _dot(sw.astype(BF16), v_ref[:, hs].astype(BF16))
        cq = [_dot_nt(q, c0_ref[bi, h].astype(BF16)) for bi in range(nseq)]
        num_inter = cq[nseq - 1]
        n_sel = jnp.broadcast_to(n0_ref[nseq - 1, h], (SUBLANES, dh))
        for bi in range(nseq - 2, -1, -1):
            num_inter = jnp.where(seq_of_row == bi, cq[bi], num_inter)
            n_sel = jnp.where(seq_of_row == bi, n0_ref[bi, h], n_sel)
        num = a * num_inter + intra
        den = a * jnp.sum(q.astype(F32) * n_sel, axis=1, keepdims=True) + jnp.sum(sw, axis=1, keepdims=True)
        ha_ref[pl.ds(r8, SUBLANES), hs] = _head_out(num, den, mrow, o, hnorm_ref[h]).astype(ha_ref.dtype)
        m_new = per_seq(lambda bi: mrow[(bi + 1) * s - 1:(bi + 1) * s, :])
        a_end = jnp.exp(tot + m_prev - m_new)
        decay = jnp.where(inblk_row, jnp.exp(to_row(tot) - brow + irow - to_row(m_new)), 0.0)
        vt_h = vt_sc[hs, :].astype(F32)
        coefs = [jnp.where((lane_row >> shift) == bi, decay, 0.0) for bi in range(nseq)]
        coef_rows = jnp.zeros((SUBLANES, LANES), F32)
        for bi in range(nseq):
            coef_rows = jnp.where(sub == bi, coefs[bi], coef_rows)
        n_upd = _dot(coef_rows.astype(BF16), ks_h)
        for bi in range(nseq):
            a_e = a_end[bi * s:bi * s + 1, :]
            c_ref[bi, h] = a_e * c0_ref[bi, h] + _dot((vt_h * coefs[bi]).astype(BF16), ks_h)
            n_ref[bi, h] = a_e * n0_ref[bi, h] + n_upd[bi:bi + 1, :]
            m_ref[bi, h] = jnp.broadcast_to(m_new[bi * s:bi * s + 1, :], (1, LANES))


def _mlstm_sample(proj, gcol, grow, m0tok, c0, n0, hnorm, *, b, s, nh, dh):
    assert SUBLANES % s == 0 and s & (s - 1) == 0, s
    t = b * s
    tb = _tile(t, LANES)
    assert tb == LANES, t
    nseq = SUBLANES // s
    spb = tb // SUBLANES
    da = nh * dh
    blk = lambda off: pl.BlockSpec((tb, da), lambda j: (j // spb, off))
    state = lambda last2: pl.BlockSpec((nseq, nh) + last2, lambda j: (j, 0, 0, 0))
    return pl.pallas_call(
        functools.partial(_mlstm_sample_body, s=s, nh=nh, dh=dh, scale=dh ** -0.5, steps_per_block=spb),
        grid=(t // SUBLANES,),
        in_specs=[
            blk(0), blk(1), blk(2), blk(3),
            pl.BlockSpec((tb, LANES), lambda j: (j // spb, 0)),
            pl.BlockSpec((SUBLANES, tb), lambda j: (0, j // spb)),
            pl.BlockSpec((tb, LANES), lambda j: (j // spb, 0)),
            state((dh, dh)), state((1, dh)),
            pl.BlockSpec((nh, 1, dh), lambda j: (0, 0, 0)),
        ],
        out_specs=[
            pl.BlockSpec((tb, da), lambda j: (j // spb, 0)),
            state((dh, dh)), state((1, dh)), state((1, LANES)),
        ],
        out_shape=[
            jax.ShapeDtypeStruct((t, da), F32),
            jax.ShapeDtypeStruct((b, nh, dh, dh), F32),
            jax.ShapeDtypeStruct((b, nh, 1, dh), F32),
            jax.ShapeDtypeStruct((b, nh, 1, LANES), F32),
        ],
        scratch_shapes=[pltpu.VMEM((da, tb), BF16), pltpu.VMEM((tb, da), BF16)],
        compiler_params=_cparams("arbitrary"),
        name="mlstm_sample",
    )(proj, proj, proj, proj, gcol, grow, m0tok, c0, n0, hnorm)


def _mix_out_body(*refs, tm, width, seq, has_state):
    if has_state:
        (x_ref, ha_ref, cb_ref, cc_ref, cx_ref, ga_ref, gb_ref, st_ref, cw_ref, woa_ref, wob_ref, wo_ref, g_ref,
         x1_ref, xn_ref, ut_ref, ubuf) = refs
    else:
        (x_ref, ha_ref, cb_ref, cc_ref, cx_ref, ga_ref, gb_ref, cw_ref, woa_ref, wob_ref, wo_ref, g_ref,
         x1_ref, xn_ref, ut_ref, ubuf) = refs

    @pl.when(pl.program_id(1) == 0)
    def _():
        ubuf[0:SUBLANES, :] = jnp.zeros((SUBLANES, ubuf.shape[1]), F32)

    u = cc_ref[...].astype(F32) * cx_ref[...].astype(F32)
    ubuf[SUBLANES:SUBLANES + tm, :] = u
    conv = None
    for j in range(width):
        back = width - 1 - j
        if back == 0:
            term = u
        else:
            term = ubuf[SUBLANES - back:SUBLANES - back + tm, :]
            if has_state:
                pos = lax.broadcasted_iota(jnp.int32, term.shape, 0) & (seq - 1)
                term = jnp.where(pos >= back, term, st_ref[back - 1])
        term = term * cw_ref[j:j + 1, :]
        conv = term if conv is None else conv + term
    ubuf[0:SUBLANES, :] = u[tm - SUBLANES:tm, :]
    if has_state:
        ut_ref[...] = u
    else:
        ut_ref[0] = u[tm - SUBLANES:tm, :]
    ya = _dot(ha_ref[...].astype(BF16), woa_ref[...])
    yb = _dot((cb_ref[...].astype(F32) * conv).astype(BF16), wob_ref[...])
    merged = _sigmoid(ga_ref[...].astype(F32)) * ya + _sigmoid(gb_ref[...].astype(F32)) * yb
    x1 = x_ref[...] + _dot(merged.astype(BF16), wo_ref[...])
    x1_ref[...] = x1
    xn_ref[...] = _rmsnorm(x1, g_ref[...]).astype(BF16)


def _mix_out(x, ha, proj, conv_state, cw, woa, wob, wo, g, *, b, s, da, db, width):
    t, d = x.shape
    assert width - 1 <= SUBLANES
    has_state = conv_state is not None
    if has_state:
        tm = t
        assert s & (s - 1) == 0 and tm % s == 0
        grid = (1, 1)
    else:
        tm = _tile(s, 512)
        grid = (b, s // tm)
    tiles = grid[1]
    assert da == db == d
    row = lambda c: pl.BlockSpec((tm, d), lambda i, k: (i * tiles + k, c))
    full = lambda shape: pl.BlockSpec(shape, lambda i, k: (0,) * len(shape))
    off = 4 * da // db
    in_specs = [row(0), row(0), row(off), row(off + 1), row(off + 2), row(off + 3), row(off + 4)]
    args = [x, ha, proj, proj, proj, proj, proj]
    if has_state:
        in_specs.append(pl.BlockSpec((width - 1, tm, d), lambda i, k: (0, 0, 0)))
        args.append(conv_state)
        ut_spec = pl.BlockSpec((tm, d), lambda i, k: (0, 0))
        ut_shape = jax.ShapeDtypeStruct((t, d), F32)
    else:
        ut_spec = pl.BlockSpec((1, SUBLANES, d), lambda i, k: (i, 0, 0))
        ut_shape = jax.ShapeDtypeStruct((b, SUBLANES, d), F32)
    in_specs += [full(cw.shape), full(woa.shape), full(wob.shape), full(wo.shape), full(g.shape)]
    args += [cw, woa, wob, wo, g]
    return pl.pallas_call(
        functools.partial(_mix_out_body, tm=tm, width=width, seq=s, has_state=has_state),
        grid=grid,
        in_specs=in_specs,
        out_specs=[row(0), row(0), ut_spec],
        out_shape=[jax.ShapeDtypeStruct((t, d), F32), jax.ShapeDtypeStruct((t, d), BF16), ut_shape],
        scratch_shapes=[pltpu.VMEM((tm + SUBLANES, d), F32)],
        compiler_params=_cparams("parallel", "arbitrary"),
        name="mix_out",
    )(*args)


NTOP = TOPK + 1
NTOP_PAD = -(-NTOP // SUBLANES) * SUBLANES


def _top_values(s, out_sc):
    out_sc[NTOP_PAD - SUBLANES:NTOP_PAD, :] = jnp.full((SUBLANES, s.shape[1]), NEG_INF, F32)
    cur = s
    for k in range(NTOP):
        mk = jnp.max(cur, axis=0, keepdims=True)
        out_sc[k:k + 1, :] = mk
        cur = jnp.where(cur >= mk, NEG_INF, cur)


def _peer_route_body(xn_ref, wq_ref, k1_ref, k2_ref, b1_ref, p1_ref, s2_ref, p2_ref, v1_sc, v2_sc, *, nrh, nk):
    q = _dot(xn_ref[...], wq_ref[...]).astype(BF16)
    half = q.shape[1] // (2 * nrh)
    for h in range(nrh):
        s1 = _dot_nt(k1_ref[h], q[:, (2 * h) * half:(2 * h + 1) * half])
        s2 = _dot_nt(k2_ref[h], q[:, (2 * h + 1) * half:(2 * h + 2) * half])
        _top_values(s1, v1_sc)
        _top_values(s2, v2_sc)
        v1 = v1_sc[...]
        v2 = v2_sc[...]
        cands = [v1[0:1, :] + v2]
        cands += [v1[a:a + 1, :] + v2[0:SUBLANES, :] for a in range(1, SUBLANES)]
        cands.append(v1[SUBLANES:NTOP_PAD, :] + v2[0:1, :])
        cand = jnp.concatenate(cands, axis=0)
        cur = cand
        for k in range(NTOP):
            mk = jnp.max(cur, axis=0, keepdims=True)
            if k == TOPK - 1:
                c_last = mk
            cur = jnp.where(cur >= mk, NEG_INF, cur)
        theta = 0.5 * (c_last + mk)
        best = v1[0:1, :] + v2[0:1, :]
        z = jnp.sum(jnp.where(cand >= theta, jnp.exp(cand - best), 0.0), axis=0, keepdims=True)
        b1_ref[h] = theta - s1
        p1_ref[h] = jnp.exp(s1 - v1[0:1, :]) / z
        s2_ref[h] = s2
        p2_ref[h] = jnp.exp(s2 - v2[0:1, :])


def _peer_route(xn, wq, k1, k2):
    t, d = xn.shape
    nrh, nk, _ = k1.shape
    assert nk == LANES and TOPK == 2 * SUBLANES
    tm = _tile(t, 512)
    out = pl.BlockSpec((nrh, nk, tm), lambda i: (0, 0, i))
    shape = jax.ShapeDtypeStruct((nrh, nk, t), F32)
    return pl.pallas_call(
        functools.partial(_peer_route_body, nrh=nrh, nk=nk),
        grid=(t // tm,),
        in_specs=[
            pl.BlockSpec((tm, d), lambda i: (i, 0)),
            pl.BlockSpec(wq.shape, lambda i: (0, 0)),
            pl.BlockSpec(k1.shape, lambda i: (0, 0, 0)),
            pl.BlockSpec(k2.shape, lambda i: (0, 0, 0)),
        ],
        out_specs=[out, out, out, out],
        out_shape=[shape, shape, shape, shape],
        scratch_shapes=[pltpu.VMEM((NTOP_PAD, tm), F32), pltpu.VMEM((NTOP_PAD, tm), F32)],
        compiler_params=_cparams("parallel"),
        name="peer_route",
    )(xn, wq, k1, k2)


def _peer_dense_body(xn_ref, x1_ref, u_ref, vt_ref, b1_ref, p1_ref, s2_ref, p2_ref, g_ref, y_ref,
                     at_sc, wt_sc, acc_sc, *, nrh, nk, final_norm):
    e = pl.program_id(1)

    @pl.when(e == 0)
    def _():
        acc_sc[...] = jnp.zeros_like(acc_sc)

    at_sc[...] = _dot_nt(u_ref[...], xn_ref[...])
    for il in range(u_ref.shape[0] // nk):
        gate = jnp.zeros((nk, xn_ref.shape[0]), F32)
        for h in range(nrh):
            b1 = b1_ref[h, il:il + 1, :]
            p1 = p1_ref[h, il:il + 1, :]
            gate = jnp.where(s2_ref[h] >= b1, gate + p2_ref[h] * p1, gate)
        a = at_sc[il * nk:(il + 1) * nk, :]
        act = 0.5 * a * (1.0 + lax.erf(a * (2.0 ** -0.5)))
        wt_sc[il * nk:(il + 1) * nk, :] = (act * gate).astype(BF16)
    acc_sc[...] += _dot(vt_ref[...], wt_sc[...])

    @pl.when(e == pl.num_programs(1) - 1)
    def _():
        y = x1_ref[...] + acc_sc[...].T
        y_ref[...] = _rmsnorm(y, g_ref[...]) if final_norm else y


def _peer_dense(xn, x1, u, vt, b1, p1, s2, p2, g, *, final_norm):
    t, d = xn.shape
    ne = u.shape[0]
    nrh, nk, _ = b1.shape
    tt = _tile(t, 512)
    et = _tile(ne, 1024)
    rows = et // nk
    route_e = pl.BlockSpec((nrh, rows, tt), lambda i, e: (0, e, i))
    route_t = pl.BlockSpec((nrh, nk, tt), lambda i, e: (0, 0, i))
    return pl.pallas_call(
        functools.partial(_peer_dense_body, nrh=nrh, nk=nk, final_norm=final_norm),
        grid=(t // tt, ne // et),
        in_specs=[
            pl.BlockSpec((tt, d), lambda i, e: (i, 0)),
            pl.BlockSpec((tt, d), lambda i, e: (i, 0)),
            pl.BlockSpec((et, d), lambda i, e: (e, 0)),
            pl.BlockSpec((d, et), lambda i, e: (0, e)),
            route_e, route_e, route_t, route_t,
            pl.BlockSpec((1, d), lambda i, e: (0, 0)),
        ],
        out_specs=pl.BlockSpec((tt, d), lambda i, e: (i, 0)),
        out_shape=jax.ShapeDtypeStruct((t, d), F32),
        scratch_shapes=[pltpu.VMEM((et, tt), F32), pltpu.VMEM((et, tt), BF16), pltpu.VMEM((d, tt), F32)],
        compiler_params=_cparams("parallel", "arbitrary"),
        name="peer_dense",
    )(xn, x1, u, vt, b1, p1, s2, p2, g)


def _pad_to(a, shape):
    return jnp.pad(a, [(0, n - m) for m, n in zip(a.shape, shape)])


def kernel(x_prompt, x_sample, state_mlstm_C, state_mlstm_n, state_mlstm_m, state_conv, norm_mix, w_in, b_if, head_norm, w_out_a, w_out_b, conv_w, w_o, norm_ffn, w_query, sub_keys1, sub_keys2, expert_u, expert_v, norm_final):
    depth = norm_mix.shape[0]
    bp, sp, d = x_prompt.shape
    bs, ss, _ = x_sample.shape
    nh, dh = state_mlstm_C.shape[2], state_mlstm_C.shape[3]
    da = nh * dh
    width, db = conv_w.shape[1], conv_w.shape[2]
    assert 2 * nh <= SUBLANES and width == 3

    xp = x_prompt.reshape(bp * sp, d)
    xs = x_sample.reshape(bs * ss, d)
    outs_p, outs_s = [], []
    for l in range(depth):
        gate0 = 4 * da
        w_all = jnp.concatenate([w_in[l][:, :gate0], w_in[l][:, gate0 + 2 * nh:]], axis=1).astype(BF16)
        w_if = w_in[l][:, gate0:gate0 + 2 * nh]
        wif = _pad_to(w_if, (d, LANES))
        wift = _pad_to(w_if.T, (SUBLANES, d))
        bcol = _pad_to(b_if[l][None, :], (1, LANES))
        brow = _pad_to(b_if[l][:, None], (SUBLANES, 1))
        hnorm = head_norm[l].reshape(nh, 1, dh)
        cw = _pad_to(conv_w[l], (SUBLANES, db))
        woa, wob, wo = w_out_a[l].astype(BF16), w_out_b[l].astype(BF16), w_o[l].astype(BF16)
        wq = w_query[l].astype(BF16)
        k1, k2 = sub_keys1[l].astype(BF16), sub_keys2[l].astype(BF16)
        u = expert_u[l].astype(BF16)
        vt = expert_v[l].T.astype(BF16)
        g_mix, g_ffn, g_fin = norm_mix[l][None, :], norm_ffn[l][None, :], norm_final[None, :]
        last = l == depth - 1

        proj, gcol, grow = _in_proj(xp, g_mix, w_all, wif, wift, bcol, brow, nh=nh, out_dtype=BF16)
        ha, cp, np_, mp = _mlstm_prompt(proj, gcol, grow, hnorm, b=bp, s=sp, nh=nh, dh=dh)
        x1, xn, ut = _mix_out(xp, ha.reshape(bp * sp, da), proj, None, cw, woa, wob, wo, g_ffn,
                              b=bp, s=sp, da=da, db=db, width=width)
        xp_next = _peer_dense(xn, x1, u, vt, *_peer_route(xn, wq, k1, k2), g_fin, final_norm=last)
        outs_p.append((cp, np_.reshape(bp, nh, dh), mp[:, :, 0, 0], ut[:, SUBLANES - (width - 1):, :]))

        proj, gcol, grow = _in_proj(xs, g_mix, w_all, wif, wift, bcol, brow, nh=nh, out_dtype=F32)
        m0tok = _pad_to(jnp.repeat(state_mlstm_m[l], ss, axis=0), (bs * ss, LANES))
        ha, cs, ns, ms = _mlstm_sample(proj, gcol, grow, m0tok, state_mlstm_C[l], state_mlstm_n[l].reshape(bs, nh, 1, dh),
                                       hnorm, b=bs, s=ss, nh=nh, dh=dh)
        buf = state_conv[l]
        st = jnp.stack([_pad_to(buf[:, width - 1 - back:, :][:, :ss, :], (bs, ss, db)).reshape(bs * ss, db)
                        for back in range(1, width)])
        x1, xn, us = _mix_out(xs, ha, proj, st, cw, woa, wob, wo, g_ffn, b=bs, s=ss, da=da, db=db, width=width)
        xs_next = _peer_dense(xn, x1, u, vt, *_peer_route(xn, wq, k1, k2), g_fin, final_norm=last)
        ext = jnp.concatenate([buf, us.reshape(bs, ss, db)], axis=1)
        outs_s.append((cs, ns.reshape(bs, nh, dh), ms[:, :, 0, 0], ext[:, ss:, :]))
        xp, xs = xp_next, xs_next

    stack = lambda outs, k: jnp.stack([o[k] for o in outs], 0)
    return (xp.reshape(bp, sp, d), xs.reshape(bs, ss, d),
            stack(outs_p, 0), stack(outs_p, 1), stack(outs_p, 2), stack(outs_p, 3),
            stack(outs_s, 0), stack(outs_s, 1), stack(outs_s, 2), stack(outs_s, 3))
```

```python
import functools
import math

import jax
import jax.numpy as jnp
from jax import lax
from jax.experimental import pallas as pl
from jax.experimental.pallas import tpu as pltpu

F32 = jnp.float32
BF16 = jnp.bfloat16
EPS = 1e-6
TOPK = 16
LANES = 128
SUBLANES = 8
V7X_VMEM_BYTES = 64 * 1024 * 1024
VMEM_LIMIT_BYTES = V7X_VMEM_BYTES * 7 // 8
MLSTM_CHUNK = 256
NEG_INF = float("-inf")


def _cparams(*semantics):
    return pltpu.CompilerParams(dimension_semantics=semantics, vmem_limit_bytes=VMEM_LIMIT_BYTES)


def _dot(a, b):
    return jnp.dot(a, b, preferred_element_type=F32)


def _dot_nt(a, b):
    return lax.dot_general(a, b, (((1,), (1,)), ((), ())), preferred_element_type=F32)


def _rmsnorm(x, g):
    return x * lax.rsqrt(jnp.mean(x * x, axis=-1, keepdims=True) + EPS) * g


def _sigmoid(x):
    return 1.0 / (1.0 + jnp.exp(-x))


def _log_sigmoid(x):
    return jnp.minimum(x, 0.0) - jnp.log1p(jnp.exp(-jnp.abs(x)))


def _split_bf16(x, parts):
    out = []
    for _ in range(parts - 1):
        p = x.astype(BF16)
        out.append(p)
        x = x - p.astype(F32)
    out.append(x.astype(BF16))
    return out


def _tile(n, pref):
    t = math.gcd(n, pref)
    assert t % SUBLANES == 0, (n, pref)
    return t


def _in_proj_body(x_ref, g_ref, w_ref, wif_ref, wift_ref, bcol_ref, brow_ref, proj_ref, gcol_ref, grow_ref, hn_ref, *, nh):
    @pl.when(pl.program_id(1) == 0)
    def _():
        hn = _rmsnorm(x_ref[...], g_ref[...])
        hn_ref[...] = hn.astype(BF16)
        h_hi, h_lo = _split_bf16(hn, 2)
        w_hi, w_lo = _split_bf16(wif_ref[...], 2)
        col = _dot(h_hi, w_hi) + _dot(h_hi, w_lo) + _dot(h_lo, w_hi) + bcol_ref[...]
        lane = lax.broadcasted_iota(jnp.int32, col.shape, 1)
        gcol_ref[...] = jnp.where(lane < nh, col, _log_sigmoid(col))
        t_hi, t_lo = _split_bf16(wift_ref[...], 2)
        row = _dot_nt(t_hi, h_hi) + _dot_nt(t_lo, h_hi) + _dot_nt(t_hi, h_lo) + brow_ref[...]
        sub = lax.broadcasted_iota(jnp.int32, row.shape, 0)
        grow_ref[...] = jnp.where(sub < nh, row, _log_sigmoid(row))

    proj_ref[...] = _dot(hn_ref[...], w_ref[...]).astype(proj_ref.dtype)


def _in_proj(x, g, w_all, wif, wift, bcol, brow, *, nh, out_dtype):
    t, d = x.shape
    ncol = w_all.shape[1]
    tm = _tile(t, 1024)
    tn = _tile(ncol, 1024)
    return pl.pallas_call(
        functools.partial(_in_proj_body, nh=nh),
        grid=(t // tm, ncol // tn),
        in_specs=[
            pl.BlockSpec((tm, d), lambda i, j: (i, 0)),
            pl.BlockSpec((1, d), lambda i, j: (0, 0)),
            pl.BlockSpec((d, tn), lambda i, j: (0, j)),
            pl.BlockSpec((d, LANES), lambda i, j: (0, 0)),
            pl.BlockSpec((SUBLANES, d), lambda i, j: (0, 0)),
            pl.BlockSpec((1, LANES), lambda i, j: (0, 0)),
            pl.BlockSpec((SUBLANES, 1), lambda i, j: (0, 0)),
        ],
        out_specs=[
            pl.BlockSpec((tm, tn), lambda i, j: (i, j)),
            pl.BlockSpec((tm, LANES), lambda i, j: (i, 0)),
            pl.BlockSpec((SUBLANES, tm), lambda i, j: (0, i)),
        ],
        out_shape=[
            jax.ShapeDtypeStruct((t, ncol), out_dtype),
            jax.ShapeDtypeStruct((t, LANES), F32),
            jax.ShapeDtypeStruct((SUBLANES, t), F32),
        ],
        scratch_shapes=[pltpu.VMEM((tm, d), BF16)],
        compiler_params=_cparams("parallel", "arbitrary"),
        name="in_proj",
    )(x, g, w_all, wif, wift, bcol, brow)


def _head_out(num, den, mrow, o, hnorm):
    hout = num / jnp.maximum(jnp.abs(den), jnp.exp(-mrow))
    hn = hout * lax.rsqrt(jnp.mean(hout * hout, axis=-1, keepdims=True) + EPS) * hnorm
    return _sigmoid(o) * hn


def _mlstm_prompt_body(q_ref, k_ref, v_ref, o_ref, gcol_ref, grow_ref, hnorm_ref,
                       ha_ref, c_ref, n_ref, m_ref, ct_sc, n_sc, m_sc, *, chunk, nchunks, nh, scale):
    h = pl.program_id(1)
    ct_sc[...] = jnp.zeros_like(ct_sc)
    n_sc[...] = jnp.zeros_like(n_sc)
    m_sc[...] = jnp.zeros_like(m_sc)
    row_i = lax.broadcasted_iota(jnp.int32, (chunk, chunk), 0)
    col_i = lax.broadcasted_iota(jnp.int32, (chunk, chunk), 1)
    causal = col_i <= row_i
    tril = jnp.where(causal, 1.0, 0.0).astype(BF16)
    triu = jnp.where(row_i <= col_i, 1.0, 0.0).astype(BF16)
    lane = lax.broadcasted_iota(jnp.int32, (chunk, LANES), 1)
    sub = lax.broadcasted_iota(jnp.int32, (SUBLANES, chunk), 0)
    hnorm = hnorm_ref[0]

    def step(c, carry):
        r0 = pl.multiple_of(c * chunk, chunk)
        q = q_ref[0, pl.ds(r0, chunk), :]
        ks = k_ref[0, pl.ds(r0, chunk), :].astype(F32) * scale
        v = v_ref[0, pl.ds(r0, chunk), :]
        o = o_ref[0, pl.ds(r0, chunk), :].astype(F32)
        gc = gcol_ref[0, pl.ds(r0, chunk), :]
        gr = grow_ref[:, pl.ds(r0, chunk)]
        bc = sum(_dot(tril, p) for p in _split_bf16(gc, 3))
        br = sum(_dot(p, triu) for p in _split_bf16(gr, 3))
        bcol = jnp.sum(jnp.where(lane == nh + h, bc, 0.0), axis=1, keepdims=True)
        icol = jnp.sum(jnp.where(lane == h, gc, 0.0), axis=1, keepdims=True)
        brow = jnp.sum(jnp.where(sub == nh + h, br, 0.0), axis=0, keepdims=True)
        irow = jnp.sum(jnp.where(sub == h, gr, 0.0), axis=0, keepdims=True)
        m_prev = m_sc[...]
        dmat = jnp.where(causal, bcol - brow + irow, NEG_INF)
        inter = bcol + m_prev
        mrow = jnp.maximum(inter, jnp.max(dmat, axis=1, keepdims=True))
        w = jnp.exp(dmat - mrow)
        a = jnp.exp(inter - mrow)
        kt = ks.T.astype(BF16)
        sw = _dot(q, kt) * w
        ct = ct_sc[...]
        n_prev = n_sc[...]
        num = a * _dot(q, ct.astype(BF16)) + _dot(sw.astype(BF16), v)
        den = a * jnp.sum(q.astype(F32) * n_prev, axis=1, keepdims=True) + jnp.sum(sw, axis=1, keepdims=True)
        ha_ref[0, pl.ds(r0, chunk), :] = _head_out(num, den, mrow, o, hnorm).astype(ha_ref.dtype)
        m_new = mrow[chunk - 1:chunk, :]
        b_last = bcol[chunk - 1:chunk, :]
        decay = jnp.exp(b_last - bcol + icol - m_new)
        a_end = jnp.exp(b_last + m_prev - m_new)
        ct_sc[...] = a_end * ct + _dot(kt, (decay * v.astype(F32)).astype(BF16))
        n_sc[...] = a_end * n_prev + jnp.sum(decay * ks, axis=0, keepdims=True)
        m_sc[...] = m_new
        return carry

    lax.fori_loop(0, nchunks, step, 0)
    c_ref[0, 0] = ct_sc[...].T
    n_ref[0, 0] = n_sc[...]
    m_ref[0, 0] = jnp.broadcast_to(m_sc[...], (1, LANES))


def _mlstm_prompt(proj, gcol, grow, hnorm, *, b, s, nh, dh):
    chunk = _tile(s, MLSTM_CHUNK)
    proj3 = proj.reshape(b, s, proj.shape[-1])
    gcol3 = gcol.reshape(b, s, LANES)
    blk = lambda off: pl.BlockSpec((1, s, dh), lambda i, h: (i, 0, off * nh + h))
    return pl.pallas_call(
        functools.partial(_mlstm_prompt_body, chunk=chunk, nchunks=s // chunk, nh=nh, scale=dh ** -0.5),
        grid=(b, nh),
        in_specs=[
            blk(0), blk(1), blk(2), blk(3),
            pl.BlockSpec((1, s, LANES), lambda i, h: (i, 0, 0)),
            pl.BlockSpec((SUBLANES, s), lambda i, h: (0, i)),
            pl.BlockSpec((1, 1, dh), lambda i, h: (h, 0, 0)),
        ],
        out_specs=[
            pl.BlockSpec((1, s, dh), lambda i, h: (i, 0, h)),
            pl.BlockSpec((1, 1, dh, dh), lambda i, h: (i, h, 0, 0)),
            pl.BlockSpec((1, 1, 1, dh), lambda i, h: (i, h, 0, 0)),
            pl.BlockSpec((1, 1, 1, LANES), lambda i, h: (i, h, 0, 0)),
        ],
        out_shape=[
            jax.ShapeDtypeStruct((b, s, nh * dh), BF16),
            jax.ShapeDtypeStruct((b, nh, dh, dh), F32),
            jax.ShapeDtypeStruct((b, nh, 1, dh), F32),
            jax.ShapeDtypeStruct((b, nh, 1, LANES), F32),
        ],
        scratch_shapes=[pltpu.VMEM((dh, dh), F32), pltpu.VMEM((1, dh), F32), pltpu.VMEM((1, 1), F32)],
        compiler_params=_cparams("parallel", "parallel"),
        name="mlstm_prompt",
    )(proj3, proj3, proj3, proj3, gcol3, grow, hnorm)


def _mlstm_sample_body(q_ref, k_ref, v_ref, o_ref, gcol_ref, grow_ref, m0_ref, c0_ref, n0_ref, hnorm_ref,
                       ha_ref, c_ref, n_ref, m_ref, vt_sc, ks_sc, *, s, nh, dh, scale, steps_per_block):
    nseq = SUBLANES // s
    shift = s.bit_length() - 1
    jj = pl.program_id(0) % steps_per_block

    @pl.when(jj == 0)
    def _():
        vt_sc[...] = v_ref[...].astype(F32).T.astype(BF16)
        ks_sc[...] = (k_ref[...].astype(F32) * scale).astype(BF16)

    r8 = pl.multiple_of(jj * SUBLANES, SUBLANES)
    lane = lax.broadcasted_iota(jnp.int32, (SUBLANES, LANES), 1) - r8
    sub = lax.broadcasted_iota(jnp.int32, (SUBLANES, LANES), 0)
    inblk = (lane >= 0) & (lane < SUBLANES)
    same = inblk & ((lane >> shift) == (sub >> shift))
    causal = same & (lane <= sub)
    eye = lane == sub
    lane_row = lane[0:1, :]
    inblk_row = inblk[0:1, :]
    sub_col = lax.broadcasted_iota(jnp.int32, (SUBLANES, 1), 0)
    seq_of_row = lax.broadcasted_iota(jnp.int32, (SUBLANES, dh), 0) >> shift

    def to_row(col):
        return jnp.sum(jnp.where(eye, col, 0.0), axis=0, keepdims=True)

    def per_seq(col_fn):
        out = col_fn(nseq - 1)
        for bi in range(nseq - 2, -1, -1):
            out = jnp.where(sub_col >> shift == bi, col_fn(bi), out)
        return out

    gc = gcol_ref[pl.ds(r8, SUBLANES), :]
    m0 = m0_ref[pl.ds(r8, SUBLANES), :]
    for h in range(nh):
        hs = slice(h * dh, (h + 1) * dh)
        icol = gc[:, h:h + 1]
        irow = grow_ref[h:h + 1, :]
        frow = grow_ref[nh + h:nh + h + 1, :]
        m_prev = m0[:, h:h + 1]
        bcol = jnp.sum(jnp.where(causal, frow, 0.0), axis=1, keepdims=True)
        tot = jnp.sum(jnp.where(same, frow, 0.0), axis=1, keepdims=True)
        brow = to_row(bcol)
        dmat = jnp.where(causal, bcol - brow + irow, NEG_INF)
        inter = bcol + m_prev
        mrow = jnp.maximum(inter, jnp.max(dmat, axis=1, keepdims=True))
        w = jnp.exp(dmat - mrow)
        a = jnp.exp(inter - mrow)
        q = q_ref[pl.ds(r8, SUBLANES), hs].astype(BF16)
        o = o_ref[pl.ds(r8, SUBLANES), hs].astype(F32)
        ks_h = ks_sc[:, hs]
        sw = _dot_nt(q, ks_h) * w
        intra = _dot(sw.astype(BF16), v_ref[:, hs].astype(BF16))
        cq = [_dot_nt(q, c0_ref[bi, h].astype(BF16)) for bi in range(nseq)]
        num_inter = cq[nseq - 1]
        n_sel = jnp.broadcast_to(n0_ref[nseq - 1, h], (SUBLANES, dh))
        for bi in range(nseq - 2, -1, -1):
            num_inter = jnp.where(seq_of_row == bi, cq[bi], num_inter)
            n_sel = jnp.where(seq_of_row == bi, n0_ref[bi, h], n_sel)
        num = a * num_inter + intra
        den = a * jnp.sum(q.astype(F32) * n_sel, axis=1, keepdims=True) + jnp.sum(sw, axis=1, keepdims=True)
        ha_ref[pl.ds(r8, SUBLANES), hs] = _head_out(num, den, mrow, o, hnorm_ref[h]).astype(ha_ref.dtype)
        m_new = per_seq(lambda bi: mrow[(bi + 1) * s - 1:(bi + 1) * s, :])
        a_end = jnp.exp(tot + m_prev - m_new)
        decay = jnp.where(inblk_row, jnp.exp(to_row(tot) - brow + irow - to_row(m_new)), 0.0)
        vt_h = vt_sc[hs, :].astype(F32)
        coefs = [jnp.where((lane_row >> shift) == bi, decay, 0.0) for bi in range(nseq)]
        coef_rows = jnp.zeros((SUBLANES, LANES), F32)
        for bi in range(nseq):
            coef_rows = jnp.where(sub == bi, coefs[bi], coef_rows)
        n_upd = _dot(coef_rows.astype(BF16), ks_h)
        for bi in range(nseq):
            a_e = a_end[bi * s:bi * s + 1, :]
            c_ref[bi, h] = a_e * c0_ref[bi, h] + _dot((vt_h * coefs[bi]).astype(BF16), ks_h)
            n_ref[bi, h] = a_e * n0_ref[bi, h] + n_upd[bi:bi + 1, :]
            m_ref[bi, h] = jnp.broadcast_to(m_new[bi * s:bi * s + 1, :], (1, LANES))


def _mlstm_sample(proj, gcol, grow, m0tok, c0, n0, hnorm, *, b, s, nh, dh):
    assert SUBLANES % s == 0 and s & (s - 1) == 0, s
    t = b * s
    tb = _tile(t, LANES)
    assert tb == LANES, t
    nseq = SUBLANES // s
    spb = tb // SUBLANES
    da = nh * dh
    blk = lambda off: pl.BlockSpec((tb, da), lambda j: (j // spb, off))
    state = lambda last2: pl.BlockSpec((nseq, nh) + last2, lambda j: (j, 0, 0, 0))
    return pl.pallas_call(
        functools.partial(_mlstm_sample_body, s=s, nh=nh, dh=dh, scale=dh ** -0.5, steps_per_block=spb),
        grid=(t // SUBLANES,),
        in_specs=[
            blk(0), blk(1), blk(2), blk(3),
            pl.BlockSpec((tb, LANES), lambda j: (j // spb, 0)),
            pl.BlockSpec((SUBLANES, tb), lambda j: (0, j // spb)),
            pl.BlockSpec((tb, LANES), lambda j: (j // spb, 0)),
            state((dh, dh)), state((1, dh)),
            pl.BlockSpec((nh, 1, dh), lambda j: (0, 0, 0)),
        ],
        out_specs=[
            pl.BlockSpec((tb, da), lambda j: (j // spb, 0)),
            state((dh, dh)), state((1, dh)), state((1, LANES)),
        ],
        out_shape=[
            jax.ShapeDtypeStruct((t, da), F32),
            jax.ShapeDtypeStruct((b, nh, dh, dh), F32),
            jax.ShapeDtypeStruct((b, nh, 1, dh), F32),
            jax.ShapeDtypeStruct((b, nh, 1, LANES), F32),
        ],
        scratch_shapes=[pltpu.VMEM((da, tb), BF16), pltpu.VMEM((tb, da), BF16)],
        compiler_params=_cparams("arbitrary"),
        name="mlstm_sample",
    )(proj, proj, proj, proj, gcol, grow, m0tok, c0, n0, hnorm)


def _mix_out_body(*refs, tm, width, seq, has_state):
    if has_state:
        (x_ref, ha_ref, cb_ref, cc_ref, cx_ref, ga_ref, gb_ref, st_ref, cw_ref, woa_ref, wob_ref, wo_ref, g_ref,
         x1_ref, xn_ref, xnt_ref, ut_ref, ubuf) = refs
    else:
        (x_ref, ha_ref, cb_ref, cc_ref, cx_ref, ga_ref, gb_ref, cw_ref, woa_ref, wob_ref, wo_ref, g_ref,
         x1_ref, xn_ref, xnt_ref, ut_ref, ubuf) = refs

    @pl.when(pl.program_id(1) == 0)
    def _():
        ubuf[0:SUBLANES, :] = jnp.zeros((SUBLANES, ubuf.shape[1]), F32)

    u = cc_ref[...].astype(F32) * cx_ref[...].astype(F32)
    ubuf[SUBLANES:SUBLANES + tm, :] = u
    conv = None
    for j in range(width):
        back = width - 1 - j
        if back == 0:
            term = u
        else:
            term = ubuf[SUBLANES - back:SUBLANES - back + tm, :]
            if has_state:
                pos = lax.broadcasted_iota(jnp.int32, term.shape, 0) & (seq - 1)
                term = jnp.where(pos >= back, term, st_ref[back - 1])
        term = term * cw_ref[j:j + 1, :]
        conv = term if conv is None else conv + term
    ubuf[0:SUBLANES, :] = u[tm - SUBLANES:tm, :]
    if has_state:
        ut_ref[...] = u
    else:
        ut_ref[0] = u[tm - SUBLANES:tm, :]
    ya = _dot(ha_ref[...].astype(BF16), woa_ref[...])
    yb = _dot((cb_ref[...].astype(F32) * conv).astype(BF16), wob_ref[...])
    merged = _sigmoid(ga_ref[...].astype(F32)) * ya + _sigmoid(gb_ref[...].astype(F32)) * yb
    x1 = x_ref[...] + _dot(merged.astype(BF16), wo_ref[...])
    x1_ref[...] = x1
    xn = _rmsnorm(x1, g_ref[...])
    xn_ref[...] = xn.astype(BF16)
    xnt_ref[...] = xn.T.astype(BF16)


def _mix_out(x, ha, proj, conv_state, cw, woa, wob, wo, g, *, b, s, da, db, width):
    t, d = x.shape
    assert width - 1 <= SUBLANES
    has_state = conv_state is not None
    if has_state:
        tm = t
        assert s & (s - 1) == 0 and tm % s == 0
        grid = (1, 1)
    else:
        tm = _tile(s, 512)
        grid = (b, s // tm)
    tiles = grid[1]
    assert da == db == d
    row = lambda c: pl.BlockSpec((tm, d), lambda i, k: (i * tiles + k, c))
    full = lambda shape: pl.BlockSpec(shape, lambda i, k: (0,) * len(shape))
    off = 4 * da // db
    in_specs = [row(0), row(0), row(off), row(off + 1), row(off + 2), row(off + 3), row(off + 4)]
    args = [x, ha, proj, proj, proj, proj, proj]
    if has_state:
        in_specs.append(pl.BlockSpec((width - 1, tm, d), lambda i, k: (0, 0, 0)))
        args.append(conv_state)
        ut_spec = pl.BlockSpec((tm, d), lambda i, k: (0, 0))
        ut_shape = jax.ShapeDtypeStruct((t, d), F32)
    else:
        ut_spec = pl.BlockSpec((1, SUBLANES, d), lambda i, k: (i, 0, 0))
        ut_shape = jax.ShapeDtypeStruct((b, SUBLANES, d), F32)
    in_specs += [full(cw.shape), full(woa.shape), full(wob.shape), full(wo.shape), full(g.shape)]
    args += [cw, woa, wob, wo, g]
    return pl.pallas_call(
        functools.partial(_mix_out_body, tm=tm, width=width, seq=s, has_state=has_state),
        grid=grid,
        in_specs=in_specs,
        out_specs=[row(0), row(0), pl.BlockSpec((d, tm), lambda i, k: (0, i * tiles + k)), ut_spec],
        out_shape=[jax.ShapeDtypeStruct((t, d), F32), jax.ShapeDtypeStruct((t, d), BF16),
                   jax.ShapeDtypeStruct((d, t), BF16), ut_shape],
        scratch_shapes=[pltpu.VMEM((tm + SUBLANES, d), F32)],
        compiler_params=_cparams("parallel", "arbitrary"),
        name="mix_out",
    )(*args)


NTOP = TOPK + 1
NTOP_PAD = -(-NTOP // SUBLANES) * SUBLANES


def _top_values(s, out_sc, want_rank):
    out_sc[NTOP_PAD - SUBLANES:NTOP_PAD, :] = jnp.full((SUBLANES, s.shape[1]), NEG_INF, F32)
    cur = s
    rank = jnp.full(s.shape, float(NTOP), F32) if want_rank else None
    for k in range(NTOP):
        mk = jnp.max(cur, axis=0, keepdims=True)
        out_sc[k:k + 1, :] = mk
        top = cur >= mk
        if want_rank:
            rank = jnp.where(top, float(k), rank)
        cur = jnp.where(top, NEG_INF, cur)
    return rank


def _peer_route_body(xn_ref, wq_ref, k1_ref, k2_ref, cnt_ref, p1_ref, r2_ref, p2_ref, v1_sc, v2_sc, *, nrh, nk):
    q = _dot(xn_ref[...], wq_ref[...]).astype(BF16)
    half = q.shape[1] // (2 * nrh)
    for h in range(nrh):
        s1 = _dot_nt(k1_ref[h], q[:, (2 * h) * half:(2 * h + 1) * half])
        s2 = _dot_nt(k2_ref[h], q[:, (2 * h + 1) * half:(2 * h + 2) * half])
        _top_values(s1, v1_sc, False)
        rank2 = _top_values(s2, v2_sc, True)
        v1 = v1_sc[...]
        v2 = v2_sc[...]
        cands = [v1[0:1, :] + v2]
        cands += [v1[a:a + 1, :] + v2[0:SUBLANES, :] for a in range(1, SUBLANES)]
        cands.append(v1[SUBLANES:NTOP_PAD, :] + v2[0:1, :])
        cand = jnp.concatenate(cands, axis=0)
        cur = cand
        for k in range(NTOP):
            mk = jnp.max(cur, axis=0, keepdims=True)
            if k == TOPK - 1:
                c_last = mk
            cur = jnp.where(cur >= mk, NEG_INF, cur)
        theta = 0.5 * (c_last + mk)
        best = v1[0:1, :] + v2[0:1, :]
        z = jnp.sum(jnp.where(cand >= theta, jnp.exp(cand - best), 0.0), axis=0, keepdims=True)
        need = theta - s1
        cnt = jnp.zeros_like(s1)
        for b in range(NTOP):
            cnt = jnp.where(v2[b:b + 1, :] >= need, cnt + 1.0, cnt)
        cnt_ref[h] = cnt
        p1_ref[h] = jnp.exp(s1 - v1[0:1, :]) * (0.5 / z)
        r2_ref[h] = rank2.astype(BF16)
        p2_ref[h] = jnp.exp(s2 - v2[0:1, :]).astype(BF16)


def _peer_route(xn, wq, k1, k2):
    t, d = xn.shape
    nrh, nk, _ = k1.shape
    assert nk == LANES and TOPK == 2 * SUBLANES
    tm = _tile(t, 512)
    out = pl.BlockSpec((nrh, nk, tm), lambda i: (0, 0, i))
    shape = jax.ShapeDtypeStruct((nrh, nk, t), F32)
    shape16 = jax.ShapeDtypeStruct((nrh, nk, t), BF16)
    return pl.pallas_call(
        functools.partial(_peer_route_body, nrh=nrh, nk=nk),
        grid=(t // tm,),
        in_specs=[
            pl.BlockSpec((tm, d), lambda i: (i, 0)),
            pl.BlockSpec(wq.shape, lambda i: (0, 0)),
            pl.BlockSpec(k1.shape, lambda i: (0, 0, 0)),
            pl.BlockSpec(k2.shape, lambda i: (0, 0, 0)),
        ],
        out_specs=[out, out, out, out],
        out_shape=[shape, shape, shape16, shape16],
        scratch_shapes=[pltpu.VMEM((NTOP_PAD, tm), F32), pltpu.VMEM((NTOP_PAD, tm), F32)],
        compiler_params=_cparams("parallel"),
        name="peer_route",
    )(xn, wq, k1, k2)


BF16_ROWS = 2 * SUBLANES


MXU_ROWS = 256


def _peer_dense_body(xnt_ref, x1_ref, u_ref, vt_ref, cnt_ref, p1_ref, r2_ref, p2_ref, g_ref, y_ref, *scratch,
                     nrh, nk, ne_tiles, final_norm):
    nchunks = u_ref.shape[0] // MXU_ROWS
    at_sc, wt_sc, acc_sc = scratch[:nchunks], scratch[nchunks:2 * nchunks], scratch[2 * nchunks]
    n = pl.program_id(0)
    tt = xnt_ref.shape[1]
    e_bc = (n + ne_tiles - 1) % ne_tiles

    @pl.when(n == 0)
    def _():
        for at in at_sc:
            at[...] = jnp.zeros_like(at)

    @pl.when((n == 0) | (e_bc == 0))
    def _():
        acc_sc[...] = jnp.zeros_like(acc_sc)

    per_chunk = MXU_ROWS // nk
    for c in range(nchunks):
        rows = slice(c * MXU_ROWS, (c + 1) * MXU_ROWS)
        for il in range(c * per_chunk, (c + 1) * per_chunk):
            cnt = [jnp.broadcast_to(cnt_ref[h, il:il + 1, :], (BF16_ROWS, tt)).astype(BF16) for h in range(nrh)]
            p1 = [jnp.broadcast_to(p1_ref[h, il:il + 1, :], (BF16_ROWS, tt)).astype(BF16) for h in range(nrh)]
            for r0 in range(0, nk, BF16_ROWS):
                grp = slice(r0, r0 + BF16_ROWS)
                gate = jnp.zeros((BF16_ROWS, tt), BF16)
                for h in range(nrh):
                    gate = jnp.where(r2_ref[h, grp, :] < cnt[h], gate + p2_ref[h, grp, :] * p1[h], gate)
                crows = slice((il - c * per_chunk) * nk + r0, (il - c * per_chunk) * nk + r0 + BF16_ROWS)
                a = at_sc[c][crows, :]
                wt_sc[c][crows, :] = (a * (1.0 + lax.erf(a * (2.0 ** -0.5))) * gate.astype(F32)).astype(BF16)
        acc_sc[...] += _dot(vt_ref[:, rows], wt_sc[c][...])
        at_sc[c][...] = _dot(u_ref[rows, :], xnt_ref[...])

    @pl.when((n >= 1) & (e_bc == ne_tiles - 1))
    def _():
        y = x1_ref[...] + acc_sc[...].T
        y_ref[...] = _rmsnorm(y, g_ref[...]) if final_norm else y


def _peer_dense(xnt, x1, u, vt, cnt, p1, r2, p2, g, *, final_norm):
    d, t = xnt.shape
    ne = u.shape[0]
    nrh, nk, _ = cnt.shape
    tt = _tile(t, 512)
    et = _tile(ne, 1024)
    assert et % MXU_ROWS == 0 and MXU_ROWS % nk == 0
    ne_tiles = ne // et
    npairs = (t // tt) * ne_tiles
    rows = et // nk

    def stage(lag):
        def pair(n):
            m = jnp.clip(n - lag, 0, npairs - 1)
            return m // ne_tiles, m % ne_tiles
        return pair

    pa, pb = stage(0), stage(1)
    route_e = pl.BlockSpec((nrh, rows, tt), lambda n: (0, pb(n)[1], pb(n)[0]))
    route_t = pl.BlockSpec((nrh, nk, tt), lambda n: (0, 0, pb(n)[0]))
    return pl.pallas_call(
        functools.partial(_peer_dense_body, nrh=nrh, nk=nk, ne_tiles=ne_tiles, final_norm=final_norm),
        grid=(npairs + 1,),
        in_specs=[
            pl.BlockSpec((d, tt), lambda n: (0, pa(n)[0])),
            pl.BlockSpec((tt, d), lambda n: (pb(n)[0], 0)),
            pl.BlockSpec((et, d), lambda n: (pa(n)[1], 0)),
            pl.BlockSpec((d, et), lambda n: (0, pb(n)[1])),
            route_e, route_e, route_t, route_t,
            pl.BlockSpec((1, d), lambda n: (0, 0)),
        ],
        out_specs=pl.BlockSpec((tt, d), lambda n: (pb(n)[0], 0)),
        out_shape=jax.ShapeDtypeStruct((t, d), F32),
        scratch_shapes=([pltpu.VMEM((MXU_ROWS, tt), F32)] * (et // MXU_ROWS)
                        + [pltpu.VMEM((MXU_ROWS, tt), BF16)] * (et // MXU_ROWS) + [pltpu.VMEM((d, tt), F32)]),
        compiler_params=_cparams("arbitrary"),
        name="peer_dense",
    )(xnt, x1, u, vt, cnt, p1, r2, p2, g)


def _pad_to(a, shape):
    return jnp.pad(a, [(0, n - m) for m, n in zip(a.shape, shape)])


def kernel(x_prompt, x_sample, state_mlstm_C, state_mlstm_n, state_mlstm_m, state_conv, norm_mix, w_in, b_if, head_norm, w_out_a, w_out_b, conv_w, w_o, norm_ffn, w_query, sub_keys1, sub_keys2, expert_u, expert_v, norm_final):
    depth = norm_mix.shape[0]
    bp, sp, d = x_prompt.shape
    bs, ss, _ = x_sample.shape
    nh, dh = state_mlstm_C.shape[2], state_mlstm_C.shape[3]
    da = nh * dh
    width, db = conv_w.shape[1], conv_w.shape[2]
    assert 2 * nh <= SUBLANES and width == 3

    xp = x_prompt.reshape(bp * sp, d)
    xs = x_sample.reshape(bs * ss, d)
    outs_p, outs_s = [], []
    for l in range(depth):
        gate0 = 4 * da
        w_all = jnp.concatenate([w_in[l][:, :gate0], w_in[l][:, gate0 + 2 * nh:]], axis=1).astype(BF16)
        w_if = w_in[l][:, gate0:gate0 + 2 * nh]
        wif = _pad_to(w_if, (d, LANES))
        wift = _pad_to(w_if.T, (SUBLANES, d))
        bcol = _pad_to(b_if[l][None, :], (1, LANES))
        brow = _pad_to(b_if[l][:, None], (SUBLANES, 1))
        hnorm = head_norm[l].reshape(nh, 1, dh)
        cw = _pad_to(conv_w[l], (SUBLANES, db))
        woa, wob, wo = w_out_a[l].astype(BF16), w_out_b[l].astype(BF16), w_o[l].astype(BF16)
        wq = w_query[l].astype(BF16)
        k1, k2 = sub_keys1[l].astype(BF16), sub_keys2[l].astype(BF16)
        u = expert_u[l].astype(BF16)
        vt = expert_v[l].T.astype(BF16)
        g_mix, g_ffn, g_fin = norm_mix[l][None, :], norm_ffn[l][None, :], norm_final[None, :]
        last = l == depth - 1

        proj, gcol, grow = _in_proj(xp, g_mix, w_all, wif, wift, bcol, brow, nh=nh, out_dtype=BF16)
        ha, cp, np_, mp = _mlstm_prompt(proj, gcol, grow, hnorm, b=bp, s=sp, nh=nh, dh=dh)
        x1, xn, xnt, ut = _mix_out(xp, ha.reshape(bp * sp, da), proj, None, cw, woa, wob, wo, g_ffn,
                                   b=bp, s=sp, da=da, db=db, width=width)
        xp_next = _peer_dense(xnt, x1, u, vt, *_peer_route(xn, wq, k1, k2), g_fin, final_norm=last)
        outs_p.append((cp, np_.reshape(bp, nh, dh), mp[:, :, 0, 0], ut[:, SUBLANES - (width - 1):, :]))

        proj, gcol, grow = _in_proj(xs, g_mix, w_all, wif, wift, bcol, brow, nh=nh, out_dtype=F32)
        m0tok = _pad_to(jnp.repeat(state_mlstm_m[l], ss, axis=0), (bs * ss, LANES))
        ha, cs, ns, ms = _mlstm_sample(proj, gcol, grow, m0tok, state_mlstm_C[l], state_mlstm_n[l].reshape(bs, nh, 1, dh),
                                       hnorm, b=bs, s=ss, nh=nh, dh=dh)
        buf = state_conv[l]
        st = jnp.stack([_pad_to(buf[:, width - 1 - back:, :][:, :ss, :], (bs, ss, db)).reshape(bs * ss, db)
                        for back in range(1, width)])
        x1, xn, xnt, us = _mix_out(xs, ha, proj, st, cw, woa, wob, wo, g_ffn, b=bs, s=ss, da=da, db=db, width=width)
        xs_next = _peer_dense(xnt, x1, u, vt, *_peer_route(xn, wq, k1, k2), g_fin, final_norm=last)
        ext = jnp.concatenate([buf, us.reshape(bs, ss, db)], axis=1)
        outs_s.append((cs, ns.reshape(bs, nh, dh), ms[:, :, 0, 0], ext[:, ss:, :]))
        xp, xs = xp_next, xs_next

    stack = lambda outs, k: jnp.stack([o[k] for o in outs], 0)
    return (xp.reshape(bp, sp, d), xs.reshape(bs, ss, d),
            stack(outs_p, 0), stack(outs_p, 1), stack(outs_p, 2), stack(outs_p, 3),
            stack(outs_s, 0), stack(outs_s, 1), stack(outs_s, 2), stack(outs_s, 3))
```
